```python
import jax, jax.numpy as jnp
from jax import lax
import numpy as np

D_MODEL = 2048
BATCH = 4
SEQ = 2048
DEPTH = 2

HEAD_DIM = 64
FOX_HEADS = D_MODEL // (4 * HEAD_DIM)
DIL_HEADS = D_MODEL // (4 * HEAD_DIM)
MLSTM_HEADS = 4
MLSTM_DV = D_MODEL // (2 * MLSTM_HEADS)
MLSTM_DQK = MLSTM_DV // 2
FOX_W = FOX_HEADS * HEAD_DIM
DIL_W = DIL_HEADS * HEAD_DIM
MLSTM_QKW = MLSTM_HEADS * MLSTM_DQK
MLSTM_W = MLSTM_HEADS * MLSTM_DV
D_MIX = FOX_W + DIL_W + MLSTM_W
D_IN = 3 * FOX_W + 3 * DIL_W + 2 * MLSTM_QKW + 2 * MLSTM_W + FOX_HEADS + 2 * MLSTM_HEADS
CONV_WIDTH = 4
MLSTM_CHUNK = 64
Q_BLOCK = 128
DIL_CONFIGS = ((128, 1), (512, 4), (2048, 16))
ROPE_THETA = 10000.0
PEER_HEADS = 8
PEER_NKEYS = 128
PEER_EXPERTS = PEER_NKEYS * PEER_NKEYS
PEER_DKEY = 256
PEER_TOPK = 16
PEER_TOKEN_BLOCK = 128
EPS = 1e-6

kernel_name = 'hybrid_fox_mlstm_dilated_peer'


def _split_points():
    sizes = (FOX_W, FOX_W, FOX_W, DIL_W, DIL_W, DIL_W, MLSTM_QKW, MLSTM_QKW,
             MLSTM_W, MLSTM_W, FOX_HEADS, MLSTM_HEADS, MLSTM_HEADS)
    return [int(p) for p in np.cumsum(sizes)[:-1]]


def _rmsnorm(x, g=None):
    xf = x.astype(jnp.float32)
    y = xf * lax.rsqrt(jnp.mean(xf * xf, axis=-1, keepdims=True) + EPS)
    if g is not None:
        y = y * g.astype(jnp.float32)
    return y.astype(x.dtype)


def _heads(t, n_heads):
    B, S, _ = t.shape
    return t.reshape(B, S, n_heads, -1).transpose(0, 2, 1, 3)


def _rope(t):
    S, hd = t.shape[2], t.shape[3]
    inv = jnp.power(ROPE_THETA, -jnp.arange(0, hd, 2, dtype=jnp.float32) / hd)
    ang = jnp.arange(S, dtype=jnp.float32)[:, None] * inv[None, :]
    cos, sin = jnp.cos(ang), jnp.sin(ang)
    tf = t.astype(jnp.float32)
    t1, t2 = tf[..., : hd // 2], tf[..., hd // 2:]
    return jnp.concatenate([t1 * cos - t2 * sin, t2 * cos + t1 * sin], axis=-1).astype(t.dtype)


def _causal_conv(t, w):
    S = t.shape[1]
    K = w.shape[0]
    tp = jnp.pad(t, ((0, 0), (K - 1, 0), (0, 0)))
    out = tp[:, 0:S] * w[0]
    for j in range(1, K):
        out = out + tp[:, j:j + S] * w[j]
    return out


def _fox_attention(q, k, v, logf):
    B, H, S, hd = q.shape
    nb = S // Q_BLOCK
    F = jnp.cumsum(logf.astype(jnp.float32), axis=-1)
    qb = jnp.moveaxis(q.reshape(B, H, nb, Q_BLOCK, hd), 2, 0)
    Fb = jnp.moveaxis(F.reshape(B, H, nb, Q_BLOCK), 2, 0)
    kpos = jnp.arange(S)

    def block(args):
        qi, Fi, i = args
        s = jnp.einsum('bhqd,bhkd->bhqk', qi, k).astype(jnp.float32)
        s = s + Fi[..., :, None] - F[..., None, :]
        qpos = i * Q_BLOCK + jnp.arange(Q_BLOCK)
        s = jnp.where(kpos[None, :] <= qpos[:, None], s, -jnp.inf)
        p = jax.nn.softmax(s, axis=-1)
        return jnp.einsum('bhqk,bhkd->bhqd', p.astype(v.dtype), v)

    o = lax.map(block, (qb, Fb, jnp.arange(nb)))
    return jnp.moveaxis(o, 0, 2).reshape(B, H, S, hd)


def _banded_stats(q, k, v, span):
    B, H, G, L, hd = q.shape
    nb = -(-L // Q_BLOCK)
    Lp = nb * Q_BLOCK
    qb = jnp.pad(q, ((0, 0), (0, 0), (0, 0), (0, Lp - L), (0, 0))).reshape(B, H, G, nb, Q_BLOCK, hd)

    def two_blocks(t):
        tb = jnp.pad(t, ((0, 0), (0, 0), (0, 0), (Q_BLOCK, Lp - L), (0, 0)))
        tb = tb.reshape(B, H, G, nb + 1, Q_BLOCK, hd)
        return jnp.concatenate([tb[:, :, :, :-1], tb[:, :, :, 1:]], axis=4)

    kb, vb = two_blocks(k), two_blocks(v)
    s = jnp.einsum('bhgnqd,bhgnkd->bhgnqk', qb, kb).astype(jnp.float32)
    start = jnp.arange(nb)[:, None, None] * Q_BLOCK
    qpos = start + jnp.arange(Q_BLOCK)[None, :, None]
    kpos = start - Q_BLOCK + jnp.arange(2 * Q_BLOCK)[None, None, :]
    dist = qpos - kpos
    mask = (dist >= 0) & (dist <= span) & (kpos >= 0)
    s = jnp.where(mask, s, -jnp.inf)
    m = jnp.max(s, axis=-1)
    p = jnp.exp(s - m[..., None])
    l = jnp.sum(p, axis=-1)
    acc = jnp.einsum('bhgnqk,bhgnkd->bhgnqd', p, vb.astype(jnp.float32))
    return (m.reshape(B, H, G, Lp)[..., :L],
            l.reshape(B, H, G, Lp)[..., :L],
            acc.reshape(B, H, G, Lp, hd)[..., :L, :])


def _dilated_attention(q, k, v):
    B, H, S, hd = q.shape
    ms, ls, accs = [], [], []
    for window, dil in DIL_CONFIGS:
        L = S // dil

        def to_res(t, dil=dil, L=L):
            return t.reshape(B, H, L, dil, hd).transpose(0, 1, 3, 2, 4)

        m, l, acc = _banded_stats(to_res(q), to_res(k), to_res(v), window // dil)
        ms.append(m.transpose(0, 1, 3, 2).reshape(B, H, S))
        ls.append(l.transpose(0, 1, 3, 2).reshape(B, H, S))
        accs.append(acc.transpose(0, 1, 3, 2, 4).reshape(B, H, S, hd))
    m_all = jnp.stack(ms)
    w = jnp.exp(m_all - jnp.max(m_all, axis=0, keepdims=True))
    den = jnp.sum(w * jnp.stack(ls), axis=0)
    num = jnp.sum(w[..., None] * jnp.stack(accs), axis=0)
    return (num / den[..., None]).astype(v.dtype)


def _mlstm(q, k, v, i_pre, f_pre):
    out_dtype = v.dtype
    B, H, S, dqk = q.shape
    dv = v.shape[-1]
    L = MLSTM_CHUNK
    nc = S // L
    f32 = jnp.float32
    q = q.astype(f32)
    k = k.astype(f32) * (dqk ** -0.5)
    v = v.astype(f32)
    i_pre = i_pre.astype(f32)
    logf = jax.nn.log_sigmoid(f_pre.astype(f32))

    def chunks(t):
        return jnp.moveaxis(t.reshape((B, H, nc, L) + t.shape[3:]), 2, 0)

    tri = jnp.tril(jnp.ones((L, L), dtype=bool))

    def step(carry, xs):
        C, n, m = carry
        qc, kc, vc, ic, fc = xs
        b = jnp.cumsum(fc, axis=-1)
        D = jnp.where(tri, b[..., :, None] - b[..., None, :] + ic[..., None, :], -jnp.inf)
        g = b + m[..., None]
        mt = jnp.maximum(g, jnp.max(D, axis=-1))
        qk_w = jnp.einsum('bhtk,bhsk->bhts', qc, kc) * jnp.exp(D - mt[..., None])
        inter = jnp.exp(g - mt)
        num = (jnp.einsum('bhts,bhsv->bhtv', qk_w, vc)
               + inter[..., None] * jnp.einsum('bhtk,bhvk->bhtv', qc, C))
        den = jnp.sum(qk_w, axis=-1) + inter * jnp.einsum('bhtk,bhk->bht', qc, n)
        h = num / jnp.maximum(jnp.abs(den), jnp.exp(-mt))[..., None]
        bL = b[..., -1]
        dec = bL[..., None] - b + ic
        m_new = jnp.maximum(bL + m, jnp.max(dec, axis=-1))
        w = jnp.exp(dec - m_new[..., None])
        keep = jnp.exp(bL + m - m_new)
        C_new = keep[..., None, None] * C + jnp.einsum('bhs,bhsv,bhsk->bhvk', w, vc, kc)
        n_new = keep[..., None] * n + jnp.einsum('bhs,bhsk->bhk', w, kc)
        return (C_new, n_new, m_new), h

    init = (jnp.zeros((B, H, dv, dqk), f32), jnp.zeros((B, H, dqk), f32), jnp.zeros((B, H), f32))
    _, h = lax.scan(step, init, (chunks(q), chunks(k), chunks(v), chunks(i_pre), chunks(logf)))
    return jnp.moveaxis(h, 0, 2).reshape(B, H, S, dv).astype(out_dtype)


def _hybrid_mixer(h, w_in, fox_fb, mlstm_ib, mlstm_fb, fox_qn, fox_kn, dil_qn, dil_kn,
                  mlstm_conv, out_norm, w_out):
    B, S, _ = h.shape
    proj = h @ w_in
    (fq, fk, fv, dq, dk, dv, mq, mk, mv, mo, ff, mi, mf) = jnp.split(proj, _split_points(), axis=-1)
    scale = HEAD_DIM ** -0.5
    fq = _rmsnorm(_heads(fq, FOX_HEADS), fox_qn) * scale
    fk = _rmsnorm(_heads(fk, FOX_HEADS), fox_kn)
    logf = jax.nn.log_sigmoid((ff + fox_fb).astype(jnp.float32)).transpose(0, 2, 1)
    o_fox = _fox_attention(fq, fk, _heads(fv, FOX_HEADS), logf)
    dq = _rope(_rmsnorm(_heads(dq, DIL_HEADS), dil_qn)) * scale
    dk = _rope(_rmsnorm(_heads(dk, DIL_HEADS), dil_kn))
    o_dil = _dilated_attention(dq, dk, _heads(dv, DIL_HEADS))
    qk = jax.nn.silu(_causal_conv(jnp.concatenate([mq, mk], axis=-1), mlstm_conv))
    mq, mk = jnp.split(qk, 2, axis=-1)
    h_m = _mlstm(_heads(mq, MLSTM_HEADS), _heads(mk, MLSTM_HEADS), _heads(mv, MLSTM_HEADS),
                 (mi + mlstm_ib).transpose(0, 2, 1), (mf + mlstm_fb).transpose(0, 2, 1))
    o_a = _rmsnorm(o_fox.transpose(0, 2, 1, 3)).reshape(B, S, FOX_W)
    o_b = _rmsnorm(o_dil.transpose(0, 2, 1, 3)).reshape(B, S, DIL_W)
    o_c = _rmsnorm(h_m.transpose(0, 2, 1, 3)).reshape(B, S, MLSTM_W) * jax.nn.sigmoid(mo)
    merged = jnp.concatenate([o_a, o_b, o_c], axis=-1) * out_norm
    return merged @ w_out


def _peer(y, wq, k1, k2, u_tab, v_tab):
    B, S, D = y.shape
    T = B * S
    yf = y.reshape(T, D)
    half = PEER_DKEY // 2
    q = (yf @ wq).reshape(T, PEER_HEADS, PEER_DKEY).astype(jnp.float32)
    s1 = jnp.einsum('thd,nd->thn', q[..., :half], k1.astype(jnp.float32))
    s2 = jnp.einsum('thd,nd->thn', q[..., half:], k2.astype(jnp.float32))
    v1, i1 = lax.top_k(s1, PEER_TOPK)
    v2, i2 = lax.top_k(s2, PEER_TOPK)
    cand = (v1[..., :, None] + v2[..., None, :]).reshape(T, PEER_HEADS, PEER_TOPK * PEER_TOPK)
    top_v, top_c = lax.top_k(cand, PEER_TOPK)
    e1 = jnp.take_along_axis(i1, top_c // PEER_TOPK, axis=-1)
    e2 = jnp.take_along_axis(i2, top_c % PEER_TOPK, axis=-1)
    experts = e1 * PEER_NKEYS + e2
    gates = jax.nn.softmax(top_v, axis=-1)
    nblk = T // PEER_TOKEN_BLOCK

    def block(args):
        xb, eb, gb = args
        a = jnp.einsum('td,thkd->thk', xb, u_tab[eb]).astype(jnp.float32)
        w = (gb * jax.nn.gelu(a)).astype(v_tab.dtype)
        return jnp.einsum('thk,thkd->td', w, v_tab[eb])

    out = lax.map(block, (yf.reshape(nblk, PEER_TOKEN_BLOCK, D),
                          experts.reshape(nblk, PEER_TOKEN_BLOCK, PEER_HEADS, PEER_TOPK),
                          gates.reshape(nblk, PEER_TOKEN_BLOCK, PEER_HEADS, PEER_TOPK)))
    return out.reshape(B, S, D).astype(y.dtype)


def setup_inputs(seed: int = 0) -> dict:
    key = jax.random.key(seed)
    ks = jax.random.split(key, 22)

    def nrm(k, shape, scale):
        return jax.random.normal(k, shape, jnp.float32) * scale

    def gain(k, shape):
        return 1.0 + 0.02 * jax.random.normal(k, shape, jnp.float32)

    return {
        'x': nrm(ks[0], (BATCH, SEQ, D_MODEL), 1.0),
        'c': nrm(ks[1], (BATCH, D_MODEL), 1.0),
        'ada_w': nrm(ks[2], (DEPTH, D_MODEL, 6 * D_MODEL), 0.5 * D_MODEL ** -0.5),
        'ada_b': nrm(ks[3], (DEPTH, 6 * D_MODEL), 0.02),
        'norm_mix': gain(ks[4], (DEPTH, D_MODEL)),
        'w_in': nrm(ks[5], (DEPTH, D_MODEL, D_IN), D_MODEL ** -0.5),
        'fox_fb': 2.0 + nrm(ks[6], (DEPTH, FOX_HEADS), 0.5),
        'mlstm_ib': nrm(ks[7], (DEPTH, MLSTM_HEADS), 0.1),
        'mlstm_fb': 3.0 + nrm(ks[8], (DEPTH, MLSTM_HEADS), 0.5),
        'fox_qn': gain(ks[9], (DEPTH, HEAD_DIM)),
        'fox_kn': gain(ks[10], (DEPTH, HEAD_DIM)),
        'dil_qn': gain(ks[11], (DEPTH, HEAD_DIM)),
        'dil_kn': gain(ks[12], (DEPTH, HEAD_DIM)),
        'mlstm_conv': nrm(ks[13], (DEPTH, CONV_WIDTH, 2 * MLSTM_QKW), CONV_WIDTH ** -0.5),
        'out_norm': gain(ks[14], (DEPTH, D_MIX)),
        'w_out': nrm(ks[15], (DEPTH, D_MIX, D_MODEL), D_MIX ** -0.5),
        'norm_ffn': gain(ks[16], (DEPTH, D_MODEL)),
        'peer_wq': nrm(ks[17], (DEPTH, D_MODEL, PEER_HEADS * PEER_DKEY), D_MODEL ** -0.5),
        'peer_k1': nrm(ks[18], (DEPTH, PEER_NKEYS, PEER_DKEY // 2), (PEER_DKEY // 2) ** -0.5),
        'peer_k2': nrm(ks[19], (DEPTH, PEER_NKEYS, PEER_DKEY // 2), (PEER_DKEY // 2) ** -0.5),
        'peer_u': nrm(ks[20], (DEPTH, PEER_EXPERTS, D_MODEL), D_MODEL ** -0.5),
        'peer_v': nrm(ks[21], (DEPTH, PEER_EXPERTS, D_MODEL), PEER_HEADS ** -0.5),
    }


def reference(x, c, ada_w, ada_b, norm_mix, w_in, fox_fb, mlstm_ib, mlstm_fb, fox_qn, fox_kn,
              dil_qn, dil_kn, mlstm_conv, out_norm, w_out, norm_ffn, peer_wq, peer_k1, peer_k2,
              peer_u, peer_v):
    cond = jax.nn.silu(c)
    for l in range(DEPTH):
        mod = cond @ ada_w[l] + ada_b[l]
        sh1, sc1, g1, sh2, sc2, g2 = [t[:, None, :] for t in jnp.split(mod, 6, axis=-1)]
        h = _rmsnorm(x, norm_mix[l]) * (1 + sc1) + sh1
        mix = _hybrid_mixer(h, w_in[l], fox_fb[l], mlstm_ib[l], mlstm_fb[l], fox_qn[l], fox_kn[l],
                            dil_qn[l], dil_kn[l], mlstm_conv[l], out_norm[l], w_out[l])
        x = x + g1 * mix
        y = _rmsnorm(x, norm_ffn[l]) * (1 + sc2) + sh2
        x = x + g2 * _peer(y, peer_wq[l], peer_k1[l], peer_k2[l], peer_u[l], peer_v[l])
    return x
```

```python
import functools

import numpy as np
import jax
import jax.numpy as jnp
from jax import lax
from jax.experimental import pallas as pl
from jax.experimental.pallas import tpu as pltpu

F32 = jnp.float32
BF16 = jnp.bfloat16

LANES = 128
VMEM_LIMIT = 56 * 1024 * 1024

HEAD_DIM = 64
FOX_HEADS = 8
DIL_HEADS = 8
MLSTM_HEADS = 4
MLSTM_DQK = 128
MLSTM_DV = 256
FOX_W = FOX_HEADS * HEAD_DIM
DIL_W = DIL_HEADS * HEAD_DIM
MLSTM_QKW = MLSTM_HEADS * MLSTM_DQK
MLSTM_W = MLSTM_HEADS * MLSTM_DV
CONV_WIDTH = 4
DIL_CONFIGS = ((128, 1), (512, 4), (2048, 16))
DIL_SPAN = 128
ROPE_THETA = 10000.0
PEER_HEADS = 8
PEER_NKEYS = 128
PEER_TOPK = 16
EPS = 1e-6
NEG = -1e30

FQ_OFF, FK_OFF, FV_OFF = 0, FOX_W, 2 * FOX_W
DQ_OFF, DK_OFF, DV_OFF = 3 * FOX_W, 3 * FOX_W + DIL_W, 3 * FOX_W + 2 * DIL_W
MQ_OFF = 3 * FOX_W + 3 * DIL_W
MK_OFF = MQ_OFF + MLSTM_QKW
MV_OFF = MK_OFF + MLSTM_QKW
MO_OFF = MV_OFF + MLSTM_W
GATE_OFF = MO_OFF + MLSTM_W
D_IN_PAD = GATE_OFF + LANES
FOXF_LANE, MI_LANE, MF_LANE = 0, FOX_HEADS, FOX_HEADS + MLSTM_HEADS
GATE_ROWS = 16

MLSTM_L = 128
FOX_TQ = 256
PEER_E1_PER_TILE = 4
PEER_TE = PEER_E1_PER_TILE * PEER_NKEYS


def _cparams(*sem):
    return pltpu.CompilerParams(dimension_semantics=sem, vmem_limit_bytes=VMEM_LIMIT)


def _dot(a, b):
    return jnp.dot(a, b, preferred_element_type=F32)


def _dot_nt(a, b):
    return lax.dot_general(a, b, (((1,), (1,)), ((), ())), preferred_element_type=F32)


def _sigmoid(x):
    return 1.0 / (1.0 + jnp.exp(-x))


def _log_sigmoid(x):
    return jnp.minimum(x, 0.0) - jnp.log1p(jnp.exp(-jnp.abs(x)))


def _pair_rms(t, first_head):
    sq = t * t
    s_all = jnp.sum(sq, axis=-1, keepdims=True)
    s0 = jnp.sum(jnp.where(first_head, sq, 0.0), axis=-1, keepdims=True)
    ms = jnp.where(first_head, s0, s_all - s0) * (1.0 / HEAD_DIM)
    return t * lax.rsqrt(ms + EPS)


def _lane_col(t, lane_iota, lane):
    return jnp.sum(jnp.where(lane_iota == lane, t, 0.0), axis=-1, keepdims=True)


def _sub_row(t, sub_iota, row):
    return jnp.sum(jnp.where(sub_iota == row, t, 0.0), axis=0, keepdims=True)


def _ada_kernel(c_ref, w_ref, b_ref, o_ref):
    c = c_ref[...]
    cond = c * _sigmoid(c)
    o_ref[...] = _dot(cond, w_ref[...]) + b_ref[...]


def _ada_mod(c_pad, ada_w, ada_b):
    depth, d, n = ada_w.shape
    rows = c_pad.shape[0]
    tn = 1024
    return pl.pallas_call(
        _ada_kernel,
        grid=(depth, n // tn),
        in_specs=[pl.BlockSpec((rows, d), lambda l, j: (0, 0)),
                  pl.BlockSpec((None, d, tn), lambda l, j: (l, 0, j)),
                  pl.BlockSpec((None, 1, tn), lambda l, j: (l, 0, j))],
        out_specs=pl.BlockSpec((None, rows, tn), lambda l, j: (l, 0, j)),
        out_shape=jax.ShapeDtypeStruct((depth, rows, n), F32),
        compiler_params=_cparams("arbitrary", "arbitrary"),
        name="ada_mod",
    )(c_pad, ada_w, ada_b.reshape(depth, 1, n))


def _inproj_kernel(x_ref, mod_ref, g_ref, w_ref, o_ref, h_ref):
    @pl.when(pl.program_id(1) == 0)
    def _():
        x = x_ref[...]
        y = x * lax.rsqrt(jnp.mean(x * x, axis=-1, keepdims=True) + EPS) * g_ref[...]
        h_ref[...] = (y * (1.0 + mod_ref[1:2, :]) + mod_ref[0:1, :]).astype(BF16)

    o_ref[...] = _dot(h_ref[...], w_ref[...])


def _inproj(x2, mod_l, gain, w_pad, seq):
    t, d = x2.shape
    n = w_pad.shape[1]
    tm, tn = 1024, 896
    per_batch = seq // tm
    return pl.pallas_call(
        _inproj_kernel,
        grid=(t // tm, n // tn),
        in_specs=[pl.BlockSpec((tm, d), lambda i, j: (i, 0)),
                  pl.BlockSpec((None, 6, d), lambda i, j: (i // per_batch, 0, 0)),
                  pl.BlockSpec((1, d), lambda i, j: (0, 0)),
                  pl.BlockSpec((d, tn), lambda i, j: (0, j))],
        out_specs=pl.BlockSpec((tm, tn), lambda i, j: (i, j)),
        out_shape=jax.ShapeDtypeStruct((t, n), F32),
        scratch_shapes=[pltpu.VMEM((tm, d), BF16)],
        compiler_params=_cparams("arbitrary", "arbitrary"),
        name="inproj",
    )(x2, mod_l, gain, w_pad)


def _split3_dot(tri, v):
    hi = v.astype(BF16)
    r1 = v - hi.astype(F32)
    mid = r1.astype(BF16)
    lo = (r1 - mid.astype(F32)).astype(BF16)
    return _dot(tri, hi) + _dot(tri, mid) + _dot(tri, lo)


def _gates_kernel(g_ref, bias_ref, cs_ref, v_ref, csr_ref, vr_ref, *, seq):
    lane = lax.broadcasted_iota(jnp.int32, (1, LANES), 1)
    linear = (lane >= MI_LANE) & (lane < MF_LANE)
    z = g_ref[...] + bias_ref[...]
    v = jnp.where(linear, z, _log_sigmoid(z))
    v_ref[...] = v
    r = lax.broadcasted_iota(jnp.int32, (LANES, LANES), 0)
    c = lax.broadcasted_iota(jnp.int32, (LANES, LANES), 1)
    tri = jnp.where(c <= r, 1.0, 0.0).astype(BF16)
    carry = jnp.zeros((1, LANES), F32)
    for blk in range(seq // LANES):
        rows = slice(blk * LANES, (blk + 1) * LANES)
        vb = v[rows, :]
        cb = _split3_dot(tri, vb) + carry
        cs_ref[rows, :] = cb
        carry = cb[LANES - 1:LANES, :]
        csr_ref[:, rows] = cb.T[0:GATE_ROWS, :]
        vr_ref[:, rows] = vb.T[0:GATE_ROWS, :]


def _gates(proj3, bias):
    b, seq, _ = proj3.shape
    col = pl.BlockSpec((None, seq, LANES), lambda i: (i, 0, 0))
    row = pl.BlockSpec((None, GATE_ROWS, seq), lambda i: (i, 0, 0))
    return pl.pallas_call(
        functools.partial(_gates_kernel, seq=seq),
        grid=(b,),
        in_specs=[pl.BlockSpec((None, seq, LANES), lambda i: (i, 0, GATE_OFF // LANES)),
                  pl.BlockSpec((1, LANES), lambda i: (0, 0))],
        out_specs=[col, col, row, row],
        out_shape=[jax.ShapeDtypeStruct((b, seq, LANES), F32),
                   jax.ShapeDtypeStruct((b, seq, LANES), F32),
                   jax.ShapeDtypeStruct((b, GATE_ROWS, seq), F32),
                   jax.ShapeDtypeStruct((b, GATE_ROWS, seq), F32)],
        compiler_params=_cparams("arbitrary"),
        name="gates",
    )(proj3, bias)


def _fox_kernel(q_ref, k_ref, v_ref, fc_ref, fr_ref, qn_ref, kn_ref, on_ref, o_ref, *, seq):
    pair = pl.program_id(1)
    lane = lax.broadcasted_iota(jnp.int32, (1, LANES), 1)
    first = lane < HEAD_DIM
    sub8 = lax.broadcasted_iota(jnp.int32, (8, 1), 0)
    q = _pair_rms(q_ref[...], first) * qn_ref[...] * (HEAD_DIM ** -0.5)
    kb = (_pair_rms(k_ref[...], first) * kn_ref[...]).astype(BF16)
    vb = v_ref[...].astype(BF16)
    for qi in range(seq // FOX_TQ):
        q0, kend = qi * FOX_TQ, (qi + 1) * FOX_TQ
        qblk = q[q0:kend, :]
        qpos = q0 + lax.broadcasted_iota(jnp.int32, (FOX_TQ, 1), 0)
        kpos = lax.broadcasted_iota(jnp.int32, (1, kend), 1)
        causal = kpos <= qpos
        outs = []
        for hh in range(2):
            head = 2 * pair + hh
            qm = jnp.where(first if hh == 0 else ~first, qblk, 0.0).astype(BF16)
            s = _dot_nt(qm, kb[0:kend, :])
            f_q = _lane_col(fc_ref[q0:kend, :], lane, FOXF_LANE + head)
            f_k = _sub_row(fr_ref[FOXF_LANE:FOXF_LANE + 8, 0:kend], sub8, head)
            s = jnp.where(causal, s + (f_q - f_k), NEG)
            m = jnp.max(s, axis=-1, keepdims=True)
            p = jnp.exp(s - m)
            l = jnp.sum(p, axis=-1, keepdims=True)
            outs.append(_dot(p.astype(BF16), vb[0:kend, :]) / l)
        o = jnp.where(first, outs[0], outs[1])
        o_ref[q0:kend, :] = (_pair_rms(o, first) * on_ref[...]).astype(BF16)


def _fox(proj3, cs_col, cs_row, qn, kn, out_norm):
    b, seq, _ = proj3.shape
    pairs = FOX_W // LANES

    def col(off):
        return pl.BlockSpec((None, seq, LANES), lambda i, p: (i, 0, off // LANES + p))

    return pl.pallas_call(
        functools.partial(_fox_kernel, seq=seq),
        grid=(b, pairs),
        in_specs=[col(FQ_OFF), col(FK_OFF), col(FV_OFF),
                  pl.BlockSpec((None, seq, LANES), lambda i, p: (i, 0, 0)),
                  pl.BlockSpec((None, GATE_ROWS, seq), lambda i, p: (i, 0, 0)),
                  pl.BlockSpec((1, LANES), lambda i, p: (0, 0)),
                  pl.BlockSpec((1, LANES), lambda i, p: (0, 0)),
                  pl.BlockSpec((1, LANES), lambda i, p: (0, p))],
        out_specs=pl.BlockSpec((None, seq, LANES), lambda i, p: (i, 0, p)),
        out_shape=jax.ShapeDtypeStruct((b, seq, FOX_W), BF16),
        compiler_params=_cparams("arbitrary", "arbitrary"),
        name="fox_attention",
    )(proj3, proj3, proj3, cs_col, cs_row, qn, kn, out_norm)


def _dil_kernel(q_ref, k_ref, v_ref, cos_ref, sa_ref, sb_ref, qn_ref, kn_ref, on_ref, o_ref,
                qs, ks, vs, ms, ls, accs, *, seq, pad):
    lane = lax.broadcasted_iota(jnp.int32, (1, LANES), 1)
    first = lane < HEAD_DIM

    def rope(t):
        return (t * cos_ref[...] + pltpu.roll(t, LANES - HEAD_DIM // 2, 1) * sa_ref[...]
                + pltpu.roll(t, HEAD_DIM // 2, 1) * sb_ref[...])

    zeros = jnp.zeros((pad, LANES), F32)
    ks[0:pad, :] = zeros
    vs[0:pad, :] = zeros
    qs[...] = rope(_pair_rms(q_ref[...], first) * qn_ref[...]) * (HEAD_DIM ** -0.5)
    ks[pad:pad + seq, :] = rope(_pair_rms(k_ref[...], first) * kn_ref[...])
    vs[pad:pad + seq, :] = v_ref[...]

    qi = lax.broadcasted_iota(jnp.int32, (LANES, 1), 0)
    ki = lax.broadcasted_iota(jnp.int32, (1, 2 * LANES), 1) - LANES
    dist = qi - ki
    band = (dist >= 0) & (dist <= DIL_SPAN)

    for cfg, (_, dil) in enumerate(DIL_CONFIGS):
        nb = (seq // dil) // LANES

        def body(idx, carry, cfg=cfg, dil=dil, nb=nb):
            r = idx // nb
            n = idx - r * nb
            q0 = n * (LANES * dil) + r
            k0 = pad + q0 - LANES * dil
            qblk = qs[pl.ds(q0, LANES, stride=dil), :]
            kblk = ks[pl.ds(k0, 2 * LANES, stride=dil), :].astype(BF16)
            vblk = vs[pl.ds(k0, 2 * LANES, stride=dil), :].astype(BF16)
            valid = band & (ki + n * LANES >= 0)
            stats = []
            for hh in range(2):
                qm = jnp.where(first if hh == 0 else ~first, qblk, 0.0).astype(BF16)
                s = jnp.where(valid, _dot_nt(qm, kblk), NEG)
                m = jnp.max(s, axis=-1, keepdims=True)
                p = jnp.exp(s - m)
                l = jnp.sum(p, axis=-1, keepdims=True)
                stats.append((m, l, _dot(p.astype(BF16), vblk)))
            rows = pl.ds(q0, LANES, stride=dil)
            ms[cfg, rows, :] = jnp.where(first, stats[0][0], stats[1][0])
            ls[cfg, rows, :] = jnp.where(first, stats[0][1], stats[1][1])
            accs[cfg, rows, :] = jnp.where(first, stats[0][2], stats[1][2])
            return carry

        lax.fori_loop(0, dil * nb, body, 0)

    m_all = jnp.maximum(jnp.maximum(ms[0], ms[1]), ms[2])
    den = jnp.zeros((seq, LANES), F32)
    num = jnp.zeros((seq, LANES), F32)
    for cfg in range(len(DIL_CONFIGS)):
        w = jnp.exp(ms[cfg] - m_all)
        den = den + w * ls[cfg]
        num = num + w * accs[cfg]
    o = num / den
    o_ref[...] = (_pair_rms(o, first) * on_ref[...]).astype(BF16)


def _rope_tables(seq):
    hd = HEAD_DIM
    inv = jnp.power(ROPE_THETA, -jnp.arange(0, hd, 2, dtype=F32) / hd)
    ang = jnp.arange(seq, dtype=F32)[:, None] * inv[None, :]
    cos, sin = jnp.cos(ang), jnp.sin(ang)
    zero = jnp.zeros_like(sin)
    cos_t = jnp.concatenate([cos, cos, cos, cos], axis=-1)
    sin_a = jnp.concatenate([-sin, zero, -sin, zero], axis=-1)
    sin_b = jnp.concatenate([zero, sin, zero, sin], axis=-1)
    return cos_t, sin_a, sin_b


def _dil(proj3, tables, qn, kn, out_norm):
    b, seq, _ = proj3.shape
    pairs = DIL_W // LANES
    pad = LANES * max(d for _, d in DIL_CONFIGS)
    ncfg = len(DIL_CONFIGS)

    def col(off):
        return pl.BlockSpec((None, seq, LANES), lambda i, p: (i, 0, off // LANES + p))

    tab = pl.BlockSpec((seq, LANES), lambda i, p: (0, 0))
    vec = pl.BlockSpec((1, LANES), lambda i, p: (0, 0))
    return pl.pallas_call(
        functools.partial(_dil_kernel, seq=seq, pad=pad),
        grid=(b, pairs),
        in_specs=[col(DQ_OFF), col(DK_OFF), col(DV_OFF), tab, tab, tab, vec, vec,
                  pl.BlockSpec((1, LANES), lambda i, p: (0, FOX_W // LANES + p))],
        out_specs=pl.BlockSpec((None, seq, LANES), lambda i, p: (i, 0, p)),
        out_shape=jax.ShapeDtypeStruct((b, seq, DIL_W), BF16),
        scratch_shapes=[pltpu.VMEM((seq, LANES), F32),
                        pltpu.VMEM((pad + seq, LANES), F32),
                        pltpu.VMEM((pad + seq, LANES), F32),
                        pltpu.VMEM((ncfg, seq, LANES), F32),
                        pltpu.VMEM((ncfg, seq, LANES), F32),
                        pltpu.VMEM((ncfg, seq, LANES), F32)],
        compiler_params=_cparams("arbitrary", "arbitrary"),
        name="dilated_attention",
    )(proj3, proj3, proj3, *tables, qn, kn, out_norm)


def _mlstm_kernel(mq_ref, mk_ref, mv_ref, mo_ref, wq_ref, wk_ref, cs_ref, v_ref, csr_ref, vr_ref,
                  on_ref, o_ref, xp, qc_s, kc_s, ct_s, *, seq):
    head = pl.program_id(1)
    L = MLSTM_L
    lane = lax.broadcasted_iota(jnp.int32, (1, LANES), 1)

    def conv_silu(src_ref, w_ref):
        xp[0:8, :] = jnp.zeros((8, LANES), F32)
        xp[8:8 + seq, :] = src_ref[...]
        acc = xp[8 - (CONV_WIDTH - 1):8 - (CONV_WIDTH - 1) + seq, :] * w_ref[0:1, :]
        for j in range(1, CONV_WIDTH):
            off = 8 - (CONV_WIDTH - 1) + j
            acc = acc + xp[off:off + seq, :] * w_ref[j:j + 1, :]
        return acc * _sigmoid(acc)

    qc_s[...] = conv_silu(mq_ref, wq_ref)
    kc_s[...] = conv_silu(mk_ref, wk_ref) * (MLSTM_DQK ** -0.5)
    ct_s[...] = jnp.zeros((MLSTM_DQK, MLSTM_DV), F32)

    sub8 = lax.broadcasted_iota(jnp.int32, (8, 1), 0)
    ti = lax.broadcasted_iota(jnp.int32, (L, L), 0)
    si = lax.broadcasted_iota(jnp.int32, (L, L), 1)
    tri = si <= ti

    def body(c, carry):
        n_st, m_st, b_prev = carry
        r0 = pl.multiple_of(c * L, L)
        rows = pl.ds(r0, L)
        qc = qc_s[rows, :]
        kc = kc_s[rows, :]
        vc = mv_ref[rows, :].astype(BF16)
        b_col = _lane_col(cs_ref[rows, :], lane, MF_LANE + head)
        i_col = _lane_col(v_ref[rows, :], lane, MI_LANE + head)
        b_row = _sub_row(csr_ref[8:16, rows], sub8, MF_LANE - 8 + head)
        i_row = _sub_row(vr_ref[8:16, rows], sub8, MI_LANE - 8 + head)
        b_end = b_col[L - 1:L, :]
        d = b_col - b_row + i_row
        dm = jnp.where(tri, d, NEG)
        g = b_col - b_prev + m_st
        mt = jnp.maximum(g, jnp.max(dm, axis=-1, keepdims=True))
        qb = qc.astype(BF16)
        qkw = _dot_nt(qb, kc.astype(BF16)) * jnp.exp(dm - mt)
        inter = jnp.exp(g - mt)
        num = _dot(qkw.astype(BF16), vc) + inter * _dot(qb, ct_s[...].astype(BF16))
        den = (jnp.sum(qkw, axis=-1, keepdims=True)
               + inter * jnp.sum(qc * n_st, axis=-1, keepdims=True))
        h = num / jnp.maximum(jnp.abs(den), jnp.exp(-mt))
        hn = h * lax.rsqrt(jnp.mean(h * h, axis=-1, keepdims=True) + EPS)
        o_ref[rows, :] = (hn * _sigmoid(mo_ref[rows, :]) * on_ref[...]).astype(BF16)
        b_last = b_end - b_prev
        dec = b_end - b_col + i_col
        m_new = jnp.maximum(b_last + m_st, jnp.max(dec, axis=0, keepdims=True))
        w = jnp.exp(dec - m_new)
        keep = jnp.exp(b_last + m_st - m_new)
        kw = kc * w
        ct_s[...] = keep * ct_s[...] + _dot(kw.T.astype(BF16), vc)
        n_new = keep * n_st + jnp.sum(kw, axis=0, keepdims=True)
        return n_new, m_new, b_end

    init = (jnp.zeros((1, MLSTM_DQK), F32), jnp.zeros((1, 1), F32), jnp.zeros((1, 1), F32))
    lax.fori_loop(0, seq // L, body, init)


def _mlstm(proj3, conv_w, cs_col, v_col, cs_row, v_row, out_norm):
    b, seq, _ = proj3.shape
    dqk, dv = MLSTM_DQK, MLSTM_DV
    gate_col = pl.BlockSpec((None, seq, LANES), lambda i, h: (i, 0, 0))
    gate_row = pl.BlockSpec((None, GATE_ROWS, seq), lambda i, h: (i, 0, 0))
    return pl.pallas_call(
        functools.partial(_mlstm_kernel, seq=seq),
        grid=(b, MLSTM_HEADS),
        in_specs=[pl.BlockSpec((None, seq, dqk), lambda i, h: (i, 0, MQ_OFF // dqk + h)),
                  pl.BlockSpec((None, seq, dqk), lambda i, h: (i, 0, MK_OFF // dqk + h)),
                  pl.BlockSpec((None, seq, dv), lambda i, h: (i, 0, MV_OFF // dv + h)),
                  pl.BlockSpec((None, seq, dv), lambda i, h: (i, 0, MO_OFF // dv + h)),
                  pl.BlockSpec((CONV_WIDTH, dqk), lambda i, h: (0, h)),
                  pl.BlockSpec((CONV_WIDTH, dqk), lambda i, h: (0, MLSTM_HEADS + h)),
                  gate_col, gate_col, gate_row, gate_row,
                  pl.BlockSpec((1, dv), lambda i, h: (0, (FOX_W + DIL_W) // dv + h))],
        out_specs=pl.BlockSpec((None, seq, dv), lambda i, h: (i, 0, h)),
        out_shape=jax.ShapeDtypeStruct((b, seq, MLSTM_W), BF16),
        scratch_shapes=[pltpu.VMEM((seq + 8, dqk), F32),
                        pltpu.VMEM((seq, dqk), F32),
                        pltpu.VMEM((seq, dqk), F32),
                        pltpu.VMEM((dqk, dv), F32)],
        compiler_params=_cparams("arbitrary", "arbitrary"),
        name="mlstm",
    )(proj3, proj3, proj3, proj3, conv_w, conv_w, cs_col, v_col, cs_row, v_row, out_norm)


def _outproj_kernel(oa_ref, ob_ref, oc_ref, w_ref, x_ref, mod_ref, o_ref):
    mix = (_dot(oa_ref[...], w_ref[0:FOX_W, :])
           + _dot(ob_ref[...], w_ref[FOX_W:FOX_W + DIL_W, :])
           + _dot(oc_ref[...], w_ref[FOX_W + DIL_W:, :]))
    o_ref[...] = x_ref[...] + mod_ref[2:3, :] * mix


def _outproj(oa, ob, oc, w_out, x2, mod_l, seq):
    t, d = x2.shape
    tm, tn = 1024, 1024
    per_batch = seq // tm
    return pl.pallas_call(
        _outproj_kernel,
        grid=(t // tm, d // tn),
        in_specs=[pl.BlockSpec((tm, FOX_W), lambda i, j: (i, 0)),
                  pl.BlockSpec((tm, DIL_W), lambda i, j: (i, 0)),
                  pl.BlockSpec((tm, MLSTM_W), lambda i, j: (i, 0)),
                  pl.BlockSpec((w_out.shape[0], tn), lambda i, j: (0, j)),
                  pl.BlockSpec((tm, tn), lambda i, j: (i, j)),
                  pl.BlockSpec((None, 6, tn), lambda i, j: (i // per_batch, 0, j))],
        out_specs=pl.BlockSpec((tm, tn), lambda i, j: (i, j)),
        out_shape=jax.ShapeDtypeStruct((t, d), F32),
        compiler_params=_cparams("arbitrary", "arbitrary"),
        name="outproj",
    )(oa, ob, oc, w_out, x2, mod_l)


def _topk_rows(s, k, payload=None):
    rows = s.shape[0]
    rid = lax.broadcasted_iota(jnp.int32, s.shape, 0)
    vals, ids = [], []
    for _ in range(k):
        m = jnp.max(s, axis=0, keepdims=True)
        pos = jnp.min(jnp.where(s == m, rid, rows), axis=0, keepdims=True)
        sel = rid == pos
        if payload is None:
            ids.append(pos)
        else:
            ids.append(jnp.max(jnp.where(sel, payload, -1), axis=0, keepdims=True))
        vals.append(m)
        s = jnp.where(sel, -jnp.inf, s)
    return jnp.concatenate(vals, axis=0), jnp.concatenate(ids, axis=0)


def _peerq_kernel(x_ref, mod_ref, g_ref, wq_ref, k1_ref, k2_ref, y_ref, idx_ref, gate_ref):
    x = x_ref[...]
    y = x * lax.rsqrt(jnp.mean(x * x, axis=-1, keepdims=True) + EPS) * g_ref[...]
    yb = (y * (1.0 + mod_ref[4:5, :]) + mod_ref[3:4, :]).astype(BF16)
    y_ref[...] = yb
    q = _dot(yb, wq_ref[...])
    k1 = k1_ref[...].astype(BF16)
    k2 = k2_ref[...].astype(BF16)
    half = k1.shape[1]
    K = PEER_TOPK
    for h in range(PEER_HEADS):
        base = h * 2 * half
        s1 = _dot_nt(k1, q[:, base:base + half].astype(BF16))
        s2 = _dot_nt(k2, q[:, base + half:base + 2 * half].astype(BF16))
        v1, i1 = _topk_rows(s1, K)
        v2, i2 = _topk_rows(s2, K)
        cand = jnp.concatenate([v1[c:c + 1, :] + v2 for c in range(K)], axis=0)
        expert = jnp.concatenate([i1[c:c + 1, :] * PEER_NKEYS + i2 for c in range(K)], axis=0)
        top_v, top_e = _topk_rows(cand, K, payload=expert)
        p = jnp.exp(top_v - top_v[0:1, :])
        idx_ref[h * K:(h + 1) * K, :] = top_e
        gate_ref[h * K:(h + 1) * K, :] = p / jnp.sum(p, axis=0, keepdims=True)


def _peerq(x2, mod_l, gain, wq, k1, k2, seq):
    t, d = x2.shape
    tm = 256
    per_batch = seq // tm
    hk = PEER_HEADS * PEER_TOPK
    return pl.pallas_call(
        _peerq_kernel,
        grid=(t // tm,),
        in_specs=[pl.BlockSpec((tm, d), lambda i: (i, 0)),
                  pl.BlockSpec((None, 6, d), lambda i: (i // per_batch, 0, 0)),
                  pl.BlockSpec((1, d), lambda i: (0, 0)),
                  pl.BlockSpec(wq.shape, lambda i: (0, 0)),
                  pl.BlockSpec(k1.shape, lambda i: (0, 0)),
                  pl.BlockSpec(k2.shape, lambda i: (0, 0))],
        out_specs=[pl.BlockSpec((tm, d), lambda i: (i, 0)),
                   pl.BlockSpec((hk, tm), lambda i: (0, i)),
                   pl.BlockSpec((hk, tm), lambda i: (0, i))],
        out_shape=[jax.ShapeDtypeStruct((t, d), BF16),
                   jax.ShapeDtypeStruct((hk, t), jnp.int32),
                   jax.ShapeDtypeStruct((hk, t), F32)],
        compiler_params=_cparams("arbitrary"),
        name="peer_retrieve",
    )(x2, mod_l, gain, wq, k1, k2)


def _wbuild_kernel(idx_ref, gate_ref, w_ref, a_s, b_s, g_s, *, tb):
    idx = idx_ref[...]
    a_s[...] = (idx >> 7).astype(F32).T
    b_s[...] = (idx & (PEER_NKEYS - 1)).astype(F32).T
    g_s[...] = gate_ref[...].T
    sub = lax.broadcasted_iota(jnp.int32, (PEER_NKEYS, LANES), 0).astype(F32)
    rows = PEER_E1_PER_TILE

    def body(t, carry):
        a = a_s[pl.ds(t, 1), :]
        b = b_s[pl.ds(t, 1), :]
        g = g_s[pl.ds(t, 1), :]
        pt = jnp.where(sub == a, 1.0, 0.0).astype(BF16)
        qt = jnp.where(sub == b, g, 0.0).astype(BF16)
        wt = _dot_nt(pt, qt)
        for k in range(PEER_NKEYS // rows):
            w_ref[k, pl.ds(t * rows, rows), :] = wt[k * rows:(k + 1) * rows, :]
        return carry

    lax.fori_loop(0, tb, body, 0)


def _wbuild(idx_t, gate_t):
    hk, t = idx_t.shape
    tb = 128
    rows = PEER_E1_PER_TILE
    tiles = PEER_NKEYS // rows
    return pl.pallas_call(
        functools.partial(_wbuild_kernel, tb=tb),
        grid=(t // tb,),
        in_specs=[pl.BlockSpec((hk, tb), lambda i: (0, i)),
                  pl.BlockSpec((hk, tb), lambda i: (0, i))],
        out_specs=pl.BlockSpec((tiles, tb * rows, LANES), lambda i: (0, i, 0)),
        out_shape=jax.ShapeDtypeStruct((tiles, t * rows, LANES), F32),
        scratch_shapes=[pltpu.VMEM((tb, hk), F32)] * 3,
        compiler_params=_cparams("arbitrary"),
        name="peer_select",
    )(idx_t, gate_t)


def _gelu_tanh(x):
    return 0.5 * x * (1.0 + jnp.tanh(np.sqrt(2.0 / np.pi).astype(np.float32) * (x + 0.044715 * (x * x * x))))


def _peer_main_kernel(y_ref, u_ref, v_ref, w_ref, x_ref, mod_ref, o_ref, acc_ref, *, tm):
    k = pl.program_id(1)

    @pl.when(k == 0)
    def _():
        acc_ref[...] = jnp.zeros_like(acc_ref)

    a = _dot_nt(y_ref[...], u_ref[...])
    rows = PEER_E1_PER_TILE
    w = jnp.concatenate([w_ref[pl.ds(r, tm, stride=rows), :] for r in range(rows)], axis=1)
    h = (_gelu_tanh(a) * w).astype(BF16)
    acc_ref[...] += _dot(h, v_ref[...])

    @pl.when(k == pl.num_programs(1) - 1)
    def _():
        o_ref[...] = x_ref[...] + mod_ref[5:6, :] * acc_ref[...]


def _peer_main(y, u_tab, v_tab, w_sel, x2, mod_l, seq):
    t, d = x2.shape
    tm = 512
    per_batch = seq // tm
    rows = PEER_E1_PER_TILE
    tiles = u_tab.shape[0] // PEER_TE
    w2 = w_sel.reshape(tiles * t * rows, LANES)
    tok_tiles = t // tm
    return pl.pallas_call(
        functools.partial(_peer_main_kernel, tm=tm),
        grid=(tok_tiles, tiles),
        in_specs=[pl.BlockSpec((tm, d), lambda i, k: (i, 0)),
                  pl.BlockSpec((PEER_TE, d), lambda i, k: (k, 0)),
                  pl.BlockSpec((PEER_TE, d), lambda i, k: (k, 0)),
                  pl.BlockSpec((tm * rows, LANES), lambda i, k: (k * tok_tiles + i, 0)),
                  pl.BlockSpec((tm, d), lambda i, k: (i, 0)),
                  pl.BlockSpec((None, 6, d), lambda i, k: (i // per_batch, 0, 0))],
        out_specs=pl.BlockSpec((tm, d), lambda i, k: (i, 0)),
        out_shape=jax.ShapeDtypeStruct((t, d), F32),
        scratch_shapes=[pltpu.VMEM((tm, d), F32)],
        compiler_params=_cparams("arbitrary", "arbitrary"),
        name="peer_dense",
    )(y, u_tab, v_tab, w2, x2, mod_l)


def kernel(x, c, ada_w, ada_b, norm_mix, w_in, fox_fb, mlstm_ib, mlstm_fb, fox_qn, fox_kn, dil_qn,
           dil_kn, mlstm_conv, out_norm, w_out, norm_ffn, peer_wq, peer_k1, peer_k2, peer_u, peer_v):
    b, seq, d = x.shape
    depth = ada_w.shape[0]
    t = b * seq
    c_pad = jnp.zeros((8, d), F32).at[:b].set(c)
    mod = _ada_mod(c_pad, ada_w, ada_b).reshape(depth, 8, 6, d)
    tables = _rope_tables(seq)
    x2 = x.reshape(t, d)
    pair = lambda g: jnp.concatenate([g, g]).reshape(1, LANES)
    for l in range(depth):
        w_pad = jnp.pad(w_in[l], ((0, 0), (0, D_IN_PAD - w_in.shape[2]))).astype(BF16)
        proj = _inproj(x2, mod[l], norm_mix[l].reshape(1, d), w_pad, seq)
        proj3 = proj.reshape(b, seq, D_IN_PAD)
        bias = jnp.zeros((1, LANES), F32).at[0, :GATE_ROWS].set(
            jnp.concatenate([fox_fb[l], mlstm_ib[l], mlstm_fb[l]]))
        cs_col, v_col, cs_row, v_row = _gates(proj3, bias)
        on = out_norm[l].reshape(1, -1)
        oa = _fox(proj3, cs_col, cs_row, pair(fox_qn[l]), pair(fox_kn[l]), on)
        ob = _dil(proj3, tables, pair(dil_qn[l]), pair(dil_kn[l]), on)
        oc = _mlstm(proj3, mlstm_conv[l], cs_col, v_col, cs_row, v_row, on)
        x2 = _outproj(oa.reshape(t, FOX_W), ob.reshape(t, DIL_W), oc.reshape(t, MLSTM_W),
                      w_out[l].astype(BF16), x2, mod[l], seq)
        y, idx_t, gate_t = _peerq(x2, mod[l], norm_ffn[l].reshape(1, d), peer_wq[l].astype(BF16),
                                  peer_k1[l], peer_k2[l], seq)
        w_sel = _wbuild(idx_t, gate_t)
        x2 = _peer_main(y, peer_u[l].astype(BF16), peer_v[l].astype(BF16), w_sel, x2, mod[l], seq)
    return x2.reshape(b, seq, d)
```

```python
import functools

import numpy as np
import jax
import jax.numpy as jnp
from jax import lax
from jax.experimental import pallas as pl
from jax.experimental.pallas import tpu as pltpu

F32 = jnp.float32
BF16 = jnp.bfloat16

LANES = 128
VMEM_LIMIT = 56 * 1024 * 1024

HEAD_DIM = 64
FOX_HEADS = 8
DIL_HEADS = 8
MLSTM_HEADS = 4
MLSTM_DQK = 128
MLSTM_DV = 256
FOX_W = FOX_HEADS * HEAD_DIM
DIL_W = DIL_HEADS * HEAD_DIM
MLSTM_QKW = MLSTM_HEADS * MLSTM_DQK
MLSTM_W = MLSTM_HEADS * MLSTM_DV
CONV_WIDTH = 4
DIL_CONFIGS = ((128, 1), (512, 4), (2048, 16))
DIL_SPAN = 128
DIL_UNROLL = 4
ROPE_THETA = 10000.0
PEER_HEADS = 8
PEER_NKEYS = 128
PEER_TOPK = 16
EPS = 1e-6
NEG = -1e30

FQ_OFF, FK_OFF, FV_OFF = 0, FOX_W, 2 * FOX_W
DQ_OFF, DK_OFF, DV_OFF = 3 * FOX_W, 3 * FOX_W + DIL_W, 3 * FOX_W + 2 * DIL_W
MQ_OFF = 3 * FOX_W + 3 * DIL_W
MK_OFF = MQ_OFF + MLSTM_QKW
MV_OFF = MK_OFF + MLSTM_QKW
MO_OFF = MV_OFF + MLSTM_W
GATE_OFF = MO_OFF + MLSTM_W
D_IN_PAD = GATE_OFF + LANES
FOXF_LANE, MI_LANE, MF_LANE = 0, FOX_HEADS, FOX_HEADS + MLSTM_HEADS
GATE_ROWS = 16

MLSTM_L = 128
FOX_TQ = 256
PEER_E1_PER_TILE = 8
WBUILD_GROUP = 8
PEER_TE = PEER_E1_PER_TILE * PEER_NKEYS


def _cparams(*sem):
    return pltpu.CompilerParams(dimension_semantics=sem, vmem_limit_bytes=VMEM_LIMIT)


def _dot(a, b):
    return jnp.dot(a, b, preferred_element_type=F32)


def _dot_nt(a, b):
    return lax.dot_general(a, b, (((1,), (1,)), ((), ())), preferred_element_type=F32)


def _sigmoid(x):
    return 1.0 / (1.0 + jnp.exp(-x))


def _log_sigmoid(x):
    return jnp.minimum(x, 0.0) - jnp.log1p(jnp.exp(-jnp.abs(x)))


def _pair_rms(t, first_head):
    sq = t * t
    s_all = jnp.sum(sq, axis=-1, keepdims=True)
    s0 = jnp.sum(jnp.where(first_head, sq, 0.0), axis=-1, keepdims=True)
    ms = jnp.where(first_head, s0, s_all - s0) * (1.0 / HEAD_DIM)
    return t * lax.rsqrt(ms + EPS)


def _lane_col(t, lane_iota, lane):
    return jnp.sum(jnp.where(lane_iota == lane, t, 0.0), axis=-1, keepdims=True)


def _sub_row(t, sub_iota, row):
    return jnp.sum(jnp.where(sub_iota == row, t, 0.0), axis=0, keepdims=True)


def _ada_kernel(c_ref, w_ref, b_ref, o_ref):
    c = c_ref[...]
    cond = c * _sigmoid(c)
    o_ref[...] = _dot(cond, w_ref[...]) + b_ref[...]


def _ada_mod(c_pad, ada_w, ada_b):
    depth, d, n = ada_w.shape
    rows = c_pad.shape[0]
    tn = 1024
    return pl.pallas_call(
        _ada_kernel,
        grid=(depth, n // tn),
        in_specs=[pl.BlockSpec((rows, d), lambda l, j: (0, 0)),
                  pl.BlockSpec((None, d, tn), lambda l, j: (l, 0, j)),
                  pl.BlockSpec((None, 1, tn), lambda l, j: (l, 0, j))],
        out_specs=pl.BlockSpec((None, rows, tn), lambda l, j: (l, 0, j)),
        out_shape=jax.ShapeDtypeStruct((depth, rows, n), F32),
        compiler_params=_cparams("arbitrary", "arbitrary"),
        name="ada_mod",
    )(c_pad, ada_w, ada_b.reshape(depth, 1, n))


def _inproj_kernel(x_ref, mod_ref, g_ref, w_ref, o_ref, h_ref):
    @pl.when(pl.program_id(1) == 0)
    def _():
        x = x_ref[...]
        y = x * lax.rsqrt(jnp.mean(x * x, axis=-1, keepdims=True) + EPS) * g_ref[...]
        h_ref[...] = (y * (1.0 + mod_ref[1:2, :]) + mod_ref[0:1, :]).astype(BF16)

    o_ref[...] = _dot(h_ref[...], w_ref[...])


def _inproj(x2, mod, gain, w_pad, seq, l):
    t, d = x2.shape
    n = w_pad.shape[2]
    tm, tn = 1024, 896
    per_batch = seq // tm
    return pl.pallas_call(
        _inproj_kernel,
        grid=(t // tm, n // tn),
        in_specs=[pl.BlockSpec((tm, d), lambda i, j: (i, 0)),
                  pl.BlockSpec((None, None, 6, d), lambda i, j: (l, i // per_batch, 0, 0)),
                  pl.BlockSpec((None, 1, d), lambda i, j: (l, 0, 0)),
                  pl.BlockSpec((None, d, tn), lambda i, j: (l, 0, j))],
        out_specs=pl.BlockSpec((tm, tn), lambda i, j: (i, j)),
        out_shape=jax.ShapeDtypeStruct((t, n), F32),
        scratch_shapes=[pltpu.VMEM((tm, d), BF16)],
        compiler_params=_cparams("arbitrary", "arbitrary"),
        name="inproj",
    )(x2, mod, gain, w_pad)


def _split3_dot(tri, v):
    hi = v.astype(BF16)
    r1 = v - hi.astype(F32)
    mid = r1.astype(BF16)
    lo = (r1 - mid.astype(F32)).astype(BF16)
    return _dot(tri, hi) + _dot(tri, mid) + _dot(tri, lo)


def _gates_kernel(g_ref, bias_ref, cs_ref, v_ref, csr_ref, vr_ref, *, seq):
    lane = lax.broadcasted_iota(jnp.int32, (1, LANES), 1)
    linear = (lane >= MI_LANE) & (lane < MF_LANE)
    z = g_ref[...] + bias_ref[...]
    v = jnp.where(linear, z, _log_sigmoid(z))
    v_ref[...] = v
    r = lax.broadcasted_iota(jnp.int32, (LANES, LANES), 0)
    c = lax.broadcasted_iota(jnp.int32, (LANES, LANES), 1)
    tri = jnp.where(c <= r, 1.0, 0.0).astype(BF16)
    carry = jnp.zeros((1, LANES), F32)
    for blk in range(seq // LANES):
        rows = slice(blk * LANES, (blk + 1) * LANES)
        vb = v[rows, :]
        cb = _split3_dot(tri, vb) + carry
        cs_ref[rows, :] = cb
        carry = cb[LANES - 1:LANES, :]
        csr_ref[:, rows] = cb.T[0:GATE_ROWS, :]
        vr_ref[:, rows] = vb.T[0:GATE_ROWS, :]


def _gates(proj3, bias, l):
    b, seq, _ = proj3.shape
    col = pl.BlockSpec((None, seq, LANES), lambda i: (i, 0, 0))
    row = pl.BlockSpec((None, GATE_ROWS, seq), lambda i: (i, 0, 0))
    return pl.pallas_call(
        functools.partial(_gates_kernel, seq=seq),
        grid=(b,),
        in_specs=[pl.BlockSpec((None, seq, LANES), lambda i: (i, 0, GATE_OFF // LANES)),
                  pl.BlockSpec((None, 1, LANES), lambda i: (l, 0, 0))],
        out_specs=[col, col, row, row],
        out_shape=[jax.ShapeDtypeStruct((b, seq, LANES), F32),
                   jax.ShapeDtypeStruct((b, seq, LANES), F32),
                   jax.ShapeDtypeStruct((b, GATE_ROWS, seq), F32),
                   jax.ShapeDtypeStruct((b, GATE_ROWS, seq), F32)],
        compiler_params=_cparams("arbitrary"),
        name="gates",
    )(proj3, bias)


def _fox_kernel(q_ref, k_ref, v_ref, fc_ref, fr_ref, qn_ref, kn_ref, on_ref, o_ref, *, seq):
    pair = pl.program_id(1)
    lane = lax.broadcasted_iota(jnp.int32, (1, LANES), 1)
    first = lane < HEAD_DIM
    sub8 = lax.broadcasted_iota(jnp.int32, (8, 1), 0)
    q = _pair_rms(q_ref[...], first) * qn_ref[...] * (HEAD_DIM ** -0.5)
    kb = (_pair_rms(k_ref[...], first) * kn_ref[...]).astype(BF16)
    vb = v_ref[...].astype(BF16)
    for qi in range(seq // FOX_TQ):
        q0, kend = qi * FOX_TQ, (qi + 1) * FOX_TQ
        qblk = q[q0:kend, :]
        qpos = q0 + lax.broadcasted_iota(jnp.int32, (FOX_TQ, 1), 0)
        kpos = lax.broadcasted_iota(jnp.int32, (1, kend), 1)
        causal = kpos <= qpos
        outs = []
        for hh in range(2):
            head = 2 * pair + hh
            qm = jnp.where(first if hh == 0 else ~first, qblk, 0.0).astype(BF16)
            s = _dot_nt(qm, kb[0:kend, :])
            f_q = _lane_col(fc_ref[q0:kend, :], lane, FOXF_LANE + head)
            f_k = _sub_row(fr_ref[FOXF_LANE:FOXF_LANE + 8, 0:kend], sub8, head)
            s = jnp.where(causal, s + (f_q - f_k), NEG)
            m = jnp.max(s, axis=-1, keepdims=True)
            p = jnp.exp(s - m)
            l = jnp.sum(p, axis=-1, keepdims=True)
            outs.append(_dot(p.astype(BF16), vb[0:kend, :]) / l)
        o = jnp.where(first, outs[0], outs[1])
        o_ref[q0:kend, :] = (_pair_rms(o, first) * on_ref[...]).astype(BF16)


def _fox(proj3, cs_col, cs_row, qn, kn, out_norm, l):
    b, seq, _ = proj3.shape
    pairs = FOX_W // LANES

    def col(off):
        return pl.BlockSpec((None, seq, LANES), lambda i, p: (i, 0, off // LANES + p))

    return pl.pallas_call(
        functools.partial(_fox_kernel, seq=seq),
        grid=(b, pairs),
        in_specs=[col(FQ_OFF), col(FK_OFF), col(FV_OFF),
                  pl.BlockSpec((None, seq, LANES), lambda i, p: (i, 0, 0)),
                  pl.BlockSpec((None, GATE_ROWS, seq), lambda i, p: (i, 0, 0)),
                  pl.BlockSpec((None, 1, LANES), lambda i, p: (l, 0, 0)),
                  pl.BlockSpec((None, 1, LANES), lambda i, p: (l, 0, 0)),
                  pl.BlockSpec((None, 1, LANES), lambda i, p: (l, 0, p))],
        out_specs=pl.BlockSpec((None, seq, LANES), lambda i, p: (i, 0, p)),
        out_shape=jax.ShapeDtypeStruct((b, seq, FOX_W), BF16),
        compiler_params=_cparams("arbitrary", "arbitrary"),
        name="fox_attention",
    )(proj3, proj3, proj3, cs_col, cs_row, qn, kn, out_norm)


def _dil_kernel(q_ref, k_ref, v_ref, cos_ref, sa_ref, sb_ref, qn_ref, kn_ref, on_ref, o_ref,
                qs, ks, vs, ms, ls, accs, *, seq, pad):
    lane = lax.broadcasted_iota(jnp.int32, (1, LANES), 1)
    first = lane < HEAD_DIM

    def rope(t):
        return (t * cos_ref[...] + pltpu.roll(t, LANES - HEAD_DIM // 2, 1) * sa_ref[...]
                + pltpu.roll(t, HEAD_DIM // 2, 1) * sb_ref[...])

    zeros = jnp.zeros((pad, LANES), F32)
    ks[0:pad, :] = zeros
    vs[0:pad, :] = zeros
    qs[...] = rope(_pair_rms(q_ref[...], first) * qn_ref[...]) * (HEAD_DIM ** -0.5)
    ks[pad:pad + seq, :] = rope(_pair_rms(k_ref[...], first) * kn_ref[...])
    vs[pad:pad + seq, :] = v_ref[...]

    qi = lax.broadcasted_iota(jnp.int32, (LANES, 1), 0)
    ki = lax.broadcasted_iota(jnp.int32, (1, 2 * LANES), 1) - LANES
    dist = qi - ki
    band = (dist >= 0) & (dist <= DIL_SPAN)

    for cfg, (_, dil) in enumerate(DIL_CONFIGS):
        nb = (seq // dil) // LANES

        def body(idx, carry, cfg=cfg, dil=dil, nb=nb):
            r = idx // nb
            n = idx - r * nb
            q0 = n * (LANES * dil) + r
            k0 = pad + q0 - LANES * dil
            qblk = qs[pl.ds(q0, LANES, stride=dil), :]
            kblk = ks[pl.ds(k0, 2 * LANES, stride=dil), :].astype(BF16)
            vblk = vs[pl.ds(k0, 2 * LANES, stride=dil), :].astype(BF16)
            valid = band & (ki + n * LANES >= 0)
            stats = []
            for hh in range(2):
                qm = jnp.where(first if hh == 0 else ~first, qblk, 0.0).astype(BF16)
                s = jnp.where(valid, _dot_nt(qm, kblk), NEG)
                m = jnp.max(s, axis=-1, keepdims=True)
                p = jnp.exp(s - m)
                l = jnp.sum(p, axis=-1, keepdims=True)
                stats.append((m, l, _dot(p.astype(BF16), vblk)))
            rows = pl.ds(q0, LANES, stride=dil)
            ms[cfg, rows, :] = jnp.where(first, stats[0][0], stats[1][0])
            ls[cfg, rows, :] = jnp.where(first, stats[0][1], stats[1][1])
            accs[cfg, rows, :] = jnp.where(first, stats[0][2], stats[1][2])
            return carry

        lax.fori_loop(0, dil * nb, body, 0, unroll=DIL_UNROLL)

    m_all = jnp.maximum(jnp.maximum(ms[0], ms[1]), ms[2])
    den = jnp.zeros((seq, LANES), F32)
    num = jnp.zeros((seq, LANES), F32)
    for cfg in range(len(DIL_CONFIGS)):
        w = jnp.exp(ms[cfg] - m_all)
        den = den + w * ls[cfg]
        num = num + w * accs[cfg]
    o = num / den
    o_ref[...] = (_pair_rms(o, first) * on_ref[...]).astype(BF16)


def _rope_tables(seq):
    hd = HEAD_DIM
    inv = jnp.power(ROPE_THETA, -jnp.arange(0, hd, 2, dtype=F32) / hd)
    ang = jnp.arange(seq, dtype=F32)[:, None] * inv[None, :]
    cos, sin = jnp.cos(ang), jnp.sin(ang)
    zero = jnp.zeros_like(sin)
    cos_t = jnp.concatenate([cos, cos, cos, cos], axis=-1)
    sin_a = jnp.concatenate([-sin, zero, -sin, zero], axis=-1)
    sin_b = jnp.concatenate([zero, sin, zero, sin], axis=-1)
    return cos_t, sin_a, sin_b


def _dil(proj3, tables, qn, kn, out_norm, l):
    b, seq, _ = proj3.shape
    pairs = DIL_W // LANES
    pad = LANES * max(d for _, d in DIL_CONFIGS)
    ncfg = len(DIL_CONFIGS)

    def col(off):
        return pl.BlockSpec((None, seq, LANES), lambda i, p: (i, 0, off // LANES + p))

    tab = pl.BlockSpec((seq, LANES), lambda i, p: (0, 0))
    vec = pl.BlockSpec((None, 1, LANES), lambda i, p: (l, 0, 0))
    return pl.pallas_call(
        functools.partial(_dil_kernel, seq=seq, pad=pad),
        grid=(b, pairs),
        in_specs=[col(DQ_OFF), col(DK_OFF), col(DV_OFF), tab, tab, tab, vec, vec,
                  pl.BlockSpec((None, 1, LANES), lambda i, p: (l, 0, FOX_W // LANES + p))],
        out_specs=pl.BlockSpec((None, seq, LANES), lambda i, p: (i, 0, p)),
        out_shape=jax.ShapeDtypeStruct((b, seq, DIL_W), BF16),
        scratch_shapes=[pltpu.VMEM((seq, LANES), F32),
                        pltpu.VMEM((pad + seq, LANES), F32),
                        pltpu.VMEM((pad + seq, LANES), F32),
                        pltpu.VMEM((ncfg, seq, LANES), F32),
                        pltpu.VMEM((ncfg, seq, LANES), F32),
                        pltpu.VMEM((ncfg, seq, LANES), F32)],
        compiler_params=_cparams("arbitrary", "arbitrary"),
        name="dilated_attention",
    )(proj3, proj3, proj3, *tables, qn, kn, out_norm)


def _mlstm_kernel(mq_ref, mk_ref, mv_ref, mo_ref, wq_ref, wk_ref, cs_ref, v_ref, csr_ref, vr_ref,
                  on_ref, o_ref, xp, qc_s, kc_s, ct_s, *, seq):
    head = pl.program_id(1)
    L = MLSTM_L
    lane = lax.broadcasted_iota(jnp.int32, (1, LANES), 1)

    def conv_silu(src_ref, w_ref):
        xp[0:8, :] = jnp.zeros((8, LANES), F32)
        xp[8:8 + seq, :] = src_ref[...]
        acc = xp[8 - (CONV_WIDTH - 1):8 - (CONV_WIDTH - 1) + seq, :] * w_ref[0:1, :]
        for j in range(1, CONV_WIDTH):
            off = 8 - (CONV_WIDTH - 1) + j
            acc = acc + xp[off:off + seq, :] * w_ref[j:j + 1, :]
        return acc * _sigmoid(acc)

    qc_s[...] = conv_silu(mq_ref, wq_ref)
    kc_s[...] = conv_silu(mk_ref, wk_ref) * (MLSTM_DQK ** -0.5)
    ct_s[...] = jnp.zeros((MLSTM_DQK, MLSTM_DV), F32)

    sub8 = lax.broadcasted_iota(jnp.int32, (8, 1), 0)
    ti = lax.broadcasted_iota(jnp.int32, (L, L), 0)
    si = lax.broadcasted_iota(jnp.int32, (L, L), 1)
    tri = si <= ti

    def body(c, carry):
        n_st, m_st, b_prev = carry
        r0 = pl.multiple_of(c * L, L)
        rows = pl.ds(r0, L)
        qc = qc_s[rows, :]
        kc = kc_s[rows, :]
        vc = mv_ref[rows, :].astype(BF16)
        b_col = _lane_col(cs_ref[rows, :], lane, MF_LANE + head)
        i_col = _lane_col(v_ref[rows, :], lane, MI_LANE + head)
        b_row = _sub_row(csr_ref[8:16, rows], sub8, MF_LANE - 8 + head)
        i_row = _sub_row(vr_ref[8:16, rows], sub8, MI_LANE - 8 + head)
        b_end = b_col[L - 1:L, :]
        d = b_col - b_row + i_row
        dm = jnp.where(tri, d, NEG)
        g = b_col - b_prev + m_st
        mt = jnp.maximum(g, jnp.max(dm, axis=-1, keepdims=True))
        qb = qc.astype(BF16)
        qkw = _dot_nt(qb, kc.astype(BF16)) * jnp.exp(dm - mt)
        inter = jnp.exp(g - mt)
        num = _dot(qkw.astype(BF16), vc) + inter * _dot(qb, ct_s[...].astype(BF16))
        den = (jnp.sum(qkw, axis=-1, keepdims=True)
               + inter * jnp.sum(qc * n_st, axis=-1, keepdims=True))
        h = num / jnp.maximum(jnp.abs(den), jnp.exp(-mt))
        hn = h * lax.rsqrt(jnp.mean(h * h, axis=-1, keepdims=True) + EPS)
        o_ref[rows, :] = (hn * _sigmoid(mo_ref[rows, :]) * on_ref[...]).astype(BF16)
        b_last = b_end - b_prev
        dec = b_end - b_col + i_col
        m_new = jnp.maximum(b_last + m_st, jnp.max(dec, axis=0, keepdims=True))
        w = jnp.exp(dec - m_new)
        keep = jnp.exp(b_last + m_st - m_new)
        kw = kc * w
        ct_s[...] = keep * ct_s[...] + _dot(kw.T.astype(BF16), vc)
        n_new = keep * n_st + jnp.sum(kw, axis=0, keepdims=True)
        return n_new, m_new, b_end

    init = (jnp.zeros((1, MLSTM_DQK), F32), jnp.zeros((1, 1), F32), jnp.zeros((1, 1), F32))
    lax.fori_loop(0, seq // L, body, init)


def _mlstm(proj3, conv_w, cs_col, v_col, cs_row, v_row, out_norm, l):
    b, seq, _ = proj3.shape
    dqk, dv = MLSTM_DQK, MLSTM_DV
    gate_col = pl.BlockSpec((None, seq, LANES), lambda i, h: (i, 0, 0))
    gate_row = pl.BlockSpec((None, GATE_ROWS, seq), lambda i, h: (i, 0, 0))
    return pl.pallas_call(
        functools.partial(_mlstm_kernel, seq=seq),
        grid=(b, MLSTM_HEADS),
        in_specs=[pl.BlockSpec((None, seq, dqk), lambda i, h: (i, 0, MQ_OFF // dqk + h)),
                  pl.BlockSpec((None, seq, dqk), lambda i, h: (i, 0, MK_OFF // dqk + h)),
                  pl.BlockSpec((None, seq, dv), lambda i, h: (i, 0, MV_OFF // dv + h)),
                  pl.BlockSpec((None, seq, dv), lambda i, h: (i, 0, MO_OFF // dv + h)),
                  pl.BlockSpec((None, CONV_WIDTH, dqk), lambda i, h: (l, 0, h)),
                  pl.BlockSpec((None, CONV_WIDTH, dqk), lambda i, h: (l, 0, MLSTM_HEADS + h)),
                  gate_col, gate_col, gate_row, gate_row,
                  pl.BlockSpec((None, 1, dv), lambda i, h: (l, 0, (FOX_W + DIL_W) // dv + h))],
        out_specs=pl.BlockSpec((None, seq, dv), lambda i, h: (i, 0, h)),
        out_shape=jax.ShapeDtypeStruct((b, seq, MLSTM_W), BF16),
        scratch_shapes=[pltpu.VMEM((seq + 8, dqk), F32),
                        pltpu.VMEM((seq, dqk), F32),
                        pltpu.VMEM((seq, dqk), F32),
                        pltpu.VMEM((dqk, dv), F32)],
        compiler_params=_cparams("arbitrary", "arbitrary"),
        name="mlstm",
    )(proj3, proj3, proj3, proj3, conv_w, conv_w, cs_col, v_col, cs_row, v_row, out_norm)


def _outproj_kernel(oa_ref, ob_ref, oc_ref, w_ref, x_ref, mod_ref, o_ref):
    mix = (_dot(oa_ref[...], w_ref[0:FOX_W, :])
           + _dot(ob_ref[...], w_ref[FOX_W:FOX_W + DIL_W, :])
           + _dot(oc_ref[...], w_ref[FOX_W + DIL_W:, :]))
    o_ref[...] = x_ref[...] + mod_ref[2:3, :] * mix


def _outproj(oa, ob, oc, w_out, x2, mod, seq, l):
    t, d = x2.shape
    tm, tn = 1024, 1024
    per_batch = seq // tm
    return pl.pallas_call(
        _outproj_kernel,
        grid=(t // tm, d // tn),
        in_specs=[pl.BlockSpec((tm, FOX_W), lambda i, j: (i, 0)),
                  pl.BlockSpec((tm, DIL_W), lambda i, j: (i, 0)),
                  pl.BlockSpec((tm, MLSTM_W), lambda i, j: (i, 0)),
                  pl.BlockSpec((None, w_out.shape[1], tn), lambda i, j: (l, 0, j)),
                  pl.BlockSpec((tm, tn), lambda i, j: (i, j)),
                  pl.BlockSpec((None, None, 6, tn), lambda i, j: (l, i // per_batch, 0, j))],
        out_specs=pl.BlockSpec((tm, tn), lambda i, j: (i, j)),
        out_shape=jax.ShapeDtypeStruct((t, d), F32),
        compiler_params=_cparams("arbitrary", "arbitrary"),
        name="outproj",
    )(oa, ob, oc, w_out, x2, mod)


def _topk_rows(s, k, payload=None):
    rows = s.shape[0]
    rid = lax.broadcasted_iota(jnp.int32, s.shape, 0).astype(F32)
    vals, ids = [], []
    for _ in range(k):
        m = jnp.max(s, axis=0, keepdims=True)
        pos = jnp.min(jnp.where(s == m, rid, float(rows)), axis=0, keepdims=True)
        sel = rid == pos
        if payload is None:
            ids.append(pos)
        else:
            ids.append(jnp.max(jnp.where(sel, payload, -1.0), axis=0, keepdims=True))
        vals.append(m)
        s = jnp.where(sel, -jnp.inf, s)
    return jnp.concatenate(vals, axis=0), jnp.concatenate(ids, axis=0)


def _peerq_kernel(x_ref, mod_ref, g_ref, wq_ref, k1_ref, k2_ref, y_ref, idx_ref, gate_ref):
    x = x_ref[...]
    y = x * lax.rsqrt(jnp.mean(x * x, axis=-1, keepdims=True) + EPS) * g_ref[...]
    yb = (y * (1.0 + mod_ref[4:5, :]) + mod_ref[3:4, :]).astype(BF16)
    y_ref[...] = yb
    q = _dot(yb, wq_ref[...])
    k1 = k1_ref[...].astype(BF16)
    k2 = k2_ref[...].astype(BF16)
    half = k1.shape[1]
    K = PEER_TOPK
    sub8k = lax.broadcasted_iota(jnp.int32, (K, 1), 0)
    for h in range(PEER_HEADS):
        base = h * 2 * half
        s1 = _dot_nt(k1, q[:, base:base + half].astype(BF16))
        s2 = _dot_nt(k2, q[:, base + half:base + 2 * half].astype(BF16))
        v1, i1 = _topk_rows(s1, K)
        v2, i2 = _topk_rows(s2, K)
        cand, expert = [], []
        for c1 in range(K // 2):
            n2 = K // (c1 + 1)
            span = -(-n2 // 8) * 8
            keep = sub8k[0:span, :] < n2
            cand.append(jnp.where(keep, v1[c1:c1 + 1, :] + v2[0:span, :], -jnp.inf))
            expert.append(i1[c1:c1 + 1, :] * PEER_NKEYS + i2[0:span, :])
        cand.append(v1[K // 2:K, :] + v2[0:1, :])
        expert.append(i1[K // 2:K, :] * PEER_NKEYS + i2[0:1, :])
        top_v, top_e = _topk_rows(jnp.concatenate(cand, axis=0), K,
                                  payload=jnp.concatenate(expert, axis=0))
        p = jnp.exp(top_v - top_v[0:1, :])
        idx_ref[h * K:(h + 1) * K, :] = top_e
        gate_ref[h * K:(h + 1) * K, :] = p / jnp.sum(p, axis=0, keepdims=True)


def _peerq(x2, mod, gain, wq, k1, k2, seq, l):
    t, d = x2.shape
    tm = 256
    per_batch = seq // tm
    hk = PEER_HEADS * PEER_TOPK
    return pl.pallas_call(
        _peerq_kernel,
        grid=(t // tm,),
        in_specs=[pl.BlockSpec((tm, d), lambda i: (i, 0)),
                  pl.BlockSpec((None, None, 6, d), lambda i: (l, i // per_batch, 0, 0)),
                  pl.BlockSpec((None, 1, d), lambda i: (l, 0, 0)),
                  pl.BlockSpec((None,) + wq.shape[1:], lambda i: (l, 0, 0)),
                  pl.BlockSpec((None,) + k1.shape[1:], lambda i: (l, 0, 0)),
                  pl.BlockSpec((None,) + k2.shape[1:], lambda i: (l, 0, 0))],
        out_specs=[pl.BlockSpec((tm, d), lambda i: (i, 0)),
                   pl.BlockSpec((hk, tm), lambda i: (0, i)),
                   pl.BlockSpec((hk, tm), lambda i: (0, i))],
        out_shape=[jax.ShapeDtypeStruct((t, d), BF16),
                   jax.ShapeDtypeStruct((hk, t), F32),
                   jax.ShapeDtypeStruct((hk, t), F32)],
        compiler_params=_cparams("arbitrary"),
        name="peer_retrieve",
    )(x2, mod, gain, wq, k1, k2)


def _wbuild_kernel(idx_ref, gate_ref, w_ref, a_s, b_s, g_s, *, tb):
    idx = idx_ref[...].T
    row = jnp.floor(idx * (1.0 / PEER_NKEYS))
    a_s[...] = row
    b_s[...] = idx - row * PEER_NKEYS
    g_s[...] = gate_ref[...].T
    sub = lax.broadcasted_iota(jnp.int32, (PEER_NKEYS, LANES), 0).astype(F32)
    rows = PEER_E1_PER_TILE

    def body(grp, carry):
        t0 = pl.multiple_of(grp * WBUILD_GROUP, WBUILD_GROUP)
        a8 = a_s[pl.ds(t0, WBUILD_GROUP), :]
        b8 = b_s[pl.ds(t0, WBUILD_GROUP), :]
        g8 = g_s[pl.ds(t0, WBUILD_GROUP), :]
        for u in range(WBUILD_GROUP):
            pt = jnp.where(sub == a8[u:u + 1, :], 1.0, 0.0).astype(BF16)
            qt = jnp.where(sub == b8[u:u + 1, :], g8[u:u + 1, :], 0.0).astype(BF16)
            wt = _dot_nt(pt, qt)
            for k in range(PEER_NKEYS // rows):
                w_ref[k, pl.ds((t0 + u) * rows, rows), :] = wt[k * rows:(k + 1) * rows, :]
        return carry

    lax.fori_loop(0, tb // WBUILD_GROUP, body, 0)


def _wbuild(idx_t, gate_t):
    hk, t = idx_t.shape
    tb = 128
    rows = PEER_E1_PER_TILE
    tiles = PEER_NKEYS // rows
    return pl.pallas_call(
        functools.partial(_wbuild_kernel, tb=tb),
        grid=(t // tb,),
        in_specs=[pl.BlockSpec((hk, tb), lambda i: (0, i)),
                  pl.BlockSpec((hk, tb), lambda i: (0, i))],
        out_specs=pl.BlockSpec((tiles, tb * rows, LANES), lambda i: (0, i, 0)),
        out_shape=jax.ShapeDtypeStruct((tiles, t * rows, LANES), F32),
        scratch_shapes=[pltpu.VMEM((tb, hk), F32)] * 3,
        compiler_params=_cparams("arbitrary"),
        name="peer_select",
    )(idx_t, gate_t)


def _gelu_tanh(x):
    return 0.5 * x * (1.0 + jnp.tanh(np.sqrt(2.0 / np.pi).astype(np.float32) * (x + 0.044715 * (x * x * x))))


def _peer_main_kernel(y_ref, u_ref, v_ref, w_ref, x_ref, mod_ref, o_ref, acc_ref, h_ref, *, tm, nk):
    k = pl.program_id(1)
    rows = PEER_E1_PER_TILE

    def activations():
        a = _dot_nt(y_ref[...], u_ref[...])
        w = jnp.concatenate([w_ref[pl.ds(r, tm, stride=rows), :] for r in range(rows)], axis=1)
        return (_gelu_tanh(a) * w).astype(BF16)

    @pl.when(k == 0)
    def _():
        h_ref[0] = activations()
        acc_ref[...] = jnp.zeros_like(acc_ref)

    for parity in range(2):
        @pl.when((k > 0) & (k < nk) & (k % 2 == parity))
        def _(parity=parity):
            part = _dot(h_ref[1 - parity], v_ref[...])
            h_ref[parity] = activations()
            acc_ref[...] += part

    @pl.when(k == nk)
    def _():
        total = acc_ref[...] + _dot(h_ref[(nk - 1) % 2], v_ref[...])
        o_ref[...] = x_ref[...] + mod_ref[5:6, :] * total


def _peer_main(y, u_tab, v_tab, w_sel, x2, mod, seq, l):
    t, d = x2.shape
    tm = 512
    per_batch = seq // tm
    rows = PEER_E1_PER_TILE
    nk = u_tab.shape[1] // PEER_TE
    w2 = w_sel.reshape(nk * t * rows, LANES)
    tok_tiles = t // tm
    return pl.pallas_call(
        functools.partial(_peer_main_kernel, tm=tm, nk=nk),
        grid=(tok_tiles, nk + 1),
        in_specs=[pl.BlockSpec((tm, d), lambda i, k: (i, 0)),
                  pl.BlockSpec((None, PEER_TE, d), lambda i, k: (l, jnp.minimum(k, nk - 1), 0)),
                  pl.BlockSpec((None, PEER_TE, d), lambda i, k: (l, jnp.maximum(k - 1, 0), 0)),
                  pl.BlockSpec((tm * rows, LANES),
                               lambda i, k: (jnp.minimum(k, nk - 1) * tok_tiles + i, 0)),
                  pl.BlockSpec((tm, d), lambda i, k: (i, 0)),
                  pl.BlockSpec((None, None, 6, d), lambda i, k: (l, i // per_batch, 0, 0))],
        out_specs=pl.BlockSpec((tm, d), lambda i, k: (i, 0)),
        out_shape=jax.ShapeDtypeStruct((t, d), F32),
        scratch_shapes=[pltpu.VMEM((tm, d), F32),
                        pltpu.VMEM((2, tm, PEER_TE), BF16)],
        compiler_params=_cparams("arbitrary", "arbitrary"),
        name="peer_dense",
    )(y, u_tab, v_tab, w2, x2, mod)


def kernel(x, c, ada_w, ada_b, norm_mix, w_in, fox_fb, mlstm_ib, mlstm_fb, fox_qn, fox_kn, dil_qn,
           dil_kn, mlstm_conv, out_norm, w_out, norm_ffn, peer_wq, peer_k1, peer_k2, peer_u, peer_v):
    b, seq, d = x.shape
    depth = ada_w.shape[0]
    t = b * seq
    c_pad = jnp.zeros((8, d), F32).at[:b].set(c)
    mod = _ada_mod(c_pad, ada_w, ada_b).reshape(depth, 8, 6, d)
    tables = _rope_tables(seq)
    x2 = x.reshape(t, d)
    pair = lambda g: jnp.concatenate([g, g], axis=-1).reshape(depth, 1, LANES)
    fqn, fkn, dqn, dkn = pair(fox_qn), pair(fox_kn), pair(dil_qn), pair(dil_kn)
    w_pad = jnp.pad(w_in, ((0, 0), (0, 0), (0, D_IN_PAD - w_in.shape[2]))).astype(BF16)
    bias = jnp.pad(jnp.concatenate([fox_fb, mlstm_ib, mlstm_fb], axis=-1),
                   ((0, 0), (0, LANES - GATE_ROWS))).reshape(depth, 1, LANES)
    on = out_norm.reshape(depth, 1, -1)
    g_mix = norm_mix.reshape(depth, 1, d)
    g_ffn = norm_ffn.reshape(depth, 1, d)
    w_out_b, wq_b = w_out.astype(BF16), peer_wq.astype(BF16)
    u_b, v_b = peer_u.astype(BF16), peer_v.astype(BF16)
    for l in range(depth):
        proj3 = _inproj(x2, mod, g_mix, w_pad, seq, l).reshape(b, seq, D_IN_PAD)
        cs_col, v_col, cs_row, v_row = _gates(proj3, bias, l)
        oa = _fox(proj3, cs_col, cs_row, fqn, fkn, on, l)
        ob = _dil(proj3, tables, dqn, dkn, on, l)
        oc = _mlstm(proj3, mlstm_conv, cs_col, v_col, cs_row, v_row, on, l)
        x2 = _outproj(oa.reshape(t, FOX_W), ob.reshape(t, DIL_W), oc.reshape(t, MLSTM_W),
                      w_out_b, x2, mod, seq, l)
        y, idx_t, gate_t = _peerq(x2, mod, g_ffn, wq_b, peer_k1, peer_k2, seq, l)
        w_sel = _wbuild(idx_t, gate_t)
        x2 = _peer_main(y, u_b, v_b, w_sel, x2, mod, seq, l)
    return x2.reshape(b, seq, d)
```
